```python
import jax, jax.numpy as jnp
from jax import lax
import numpy as np

D_MODEL = 1024
BATCH = 8
SEQ = 4096
DEPTH = 4
DEC_BATCH = 32
DEC_SEQ = 2048
PAST_LEN = 128

GRID_W = 64
HEAD_DIM = 64
D_MIX = D_MODEL
GROUP_W = D_MIX // 4
N_RWKV_HEADS = GROUP_W // HEAD_DIM
N_Q_HEADS = GROUP_W // HEAD_DIM
N_KV_HEADS = N_Q_HEADS // 2
N_NAT_HEADS = GROUP_W // HEAD_DIM
POOL_WINDOWS = (2, 4, 8, 16)
N_POOL_GROUPS = len(POOL_WINDOWS)
POOL_GROUP_DIM = GROUP_W // N_POOL_GROUPS
DECAY_LORA = 64
AAA_LORA = 64
GATE_LORA = 128
A_COLS = 3 * GROUP_W + 2 * DECAY_LORA + 2 * AAA_LORA + GATE_LORA
B_COLS = GROUP_W + 2 * N_KV_HEADS * HEAD_DIM
C_COLS = 3 * GROUP_W
D_COLS = GROUP_W
IN_COLS = A_COLS + B_COLS + C_COLS + D_COLS
Q_BLOCK = 128
ROPE_THETA = 10000.0
NAT_WIN_H = 8
NAT_WIN_W = 16
D_FF_DENSE = 2816
N_EXPERTS = 8
TOP_K = 2
D_FF_EXPERT = 3584
N_DENSE = (DEPTH + 1) // 2
N_MOE = DEPTH // 2
N_MOD = 6
NORM_EPS = 1e-6
GN_EPS = 64e-5

kernel_name = 'hybrid_bidir_encoder_parallel_groups'


def rmsnorm(x, g):
    xf = x.astype(jnp.float32)
    y = xf * lax.rsqrt(jnp.mean(xf * xf, axis=-1, keepdims=True) + NORM_EPS)
    return (y * g.astype(jnp.float32)).astype(x.dtype)


def centred_shift(p):
    zero = jnp.zeros_like(p[:, :1])
    prev = jnp.concatenate([zero, p[:, :-1]], axis=1)
    nxt = jnp.concatenate([p[:, 1:], zero], axis=1)
    return 0.5 * (prev + nxt)


def rwkv7_scan(r, w, k, v, kk, a, reverse):
    B, T, H, K = r.shape
    xs = tuple(jnp.moveaxis(t, 1, 0) for t in (r, w, k, v, kk, a))

    def step(S, inp):
        r_t, w_t, k_t, v_t, kk_t, a_t = inp
        s_kk = jnp.einsum('bhvk,bhk->bhv', S, kk_t)
        S = (S * w_t[:, :, None, :] - s_kk[..., None] * (kk_t * a_t)[:, :, None, :]
             + v_t[..., None] * k_t[:, :, None, :])
        return S, jnp.einsum('bhvk,bhk->bhv', S, r_t)

    S0 = jnp.zeros((B, H, K, K), jnp.float32)
    _, ys = lax.scan(step, S0, xs, reverse=reverse)
    return jnp.moveaxis(ys, 0, 1)


def rwkv7_mixer(p, mu, w0, w2, a0, a2, g2, k_k, k_a, r_k, lnx_w, lnx_b):
    B, T, _ = p.shape
    H, K = N_RWKV_HEADS, HEAD_DIM
    p = p.astype(jnp.float32)
    p = p + mu * (centred_shift(p) - p)
    r, k, v, wl, al, gl = jnp.split(
        p, [GROUP_W, 2 * GROUP_W, 3 * GROUP_W, 3 * GROUP_W + 2 * DECAY_LORA,
            3 * GROUP_W + 2 * DECAY_LORA + 2 * AAA_LORA], axis=-1)
    wl = wl.reshape(B, T, 2, DECAY_LORA)
    al = al.reshape(B, T, 2, AAA_LORA)
    w_log = -jax.nn.softplus(-(w0 + jnp.einsum('btdr,drc->btdc', jnp.tanh(wl), w2))) - 0.5
    decay = jnp.exp(-jnp.exp(w_log))
    a = jax.nn.sigmoid(a0 + jnp.einsum('btdr,drc->btdc', al, a2))
    g = jax.nn.sigmoid(gl) @ g2
    heads = lambda t: t.reshape(B, T, H, K)
    kk = heads(k * k_k)
    kk = kk * lax.rsqrt(jnp.maximum(jnp.sum(kk * kk, axis=-1, keepdims=True), 1e-24))
    k_dir = k[:, :, None, :] * (1.0 + (a - 1.0) * k_a)
    r_h, v_h = heads(r), heads(v)
    y = (rwkv7_scan(r_h, heads(decay[:, :, 0]), heads(k_dir[:, :, 0]), v_h, kk, heads(a[:, :, 0]), False)
         + rwkv7_scan(r_h, heads(decay[:, :, 1]), heads(k_dir[:, :, 1]), v_h, kk, heads(a[:, :, 1]), True))
    mean = jnp.mean(y, axis=-1, keepdims=True)
    var = jnp.mean(jnp.square(y - mean), axis=-1, keepdims=True)
    y = ((y - mean) * lax.rsqrt(var + GN_EPS)).reshape(B, T, GROUP_W) * lnx_w + lnx_b
    k_dir_h = k_dir.reshape(B, T, 2, H, K)
    bonus = jnp.sum(r_h[:, :, None] * k_dir_h * r_k, axis=(2, 4))[..., None] * v_h
    return (y + bonus.reshape(B, T, GROUP_W)) * g


def axial_rope(T):
    t = jnp.arange(T)
    row = (t // GRID_W).astype(jnp.float32)
    col = (t % GRID_W).astype(jnp.float32)
    n_freq = HEAD_DIM // 4
    inv = ROPE_THETA ** (-jnp.arange(n_freq, dtype=jnp.float32) / n_freq)
    ang = jnp.concatenate([row[:, None] * inv, col[:, None] * inv], axis=-1)
    return jnp.cos(ang), jnp.sin(ang)


def apply_rope(x, cos, sin):
    B, T, H, D = x.shape
    xp = x.astype(jnp.float32).reshape(B, T, H, D // 2, 2)
    x0, x1 = xp[..., 0], xp[..., 1]
    c = cos[None, :, None, :]
    s = sin[None, :, None, :]
    return jnp.stack([x0 * c - x1 * s, x0 * s + x1 * c], axis=-1).reshape(B, T, H, D)


def gqa_mixer(p, q_norm, k_norm):
    B, T, _ = p.shape
    D = HEAD_DIM
    G = N_Q_HEADS // N_KV_HEADS
    q, k, v = jnp.split(p, [GROUP_W, GROUP_W + N_KV_HEADS * D], axis=-1)
    q = rmsnorm(q.reshape(B, T, N_Q_HEADS, D), q_norm)
    k = rmsnorm(k.reshape(B, T, N_KV_HEADS, D), k_norm)
    cos, sin = axial_rope(T)
    q = apply_rope(q, cos, sin)
    k = apply_rope(k, cos, sin)
    nb = T // Q_BLOCK
    qb = q.reshape(B, nb, Q_BLOCK, N_KV_HEADS, G, D).transpose(1, 0, 3, 4, 2, 5)
    kt = k.transpose(0, 2, 1, 3)
    vt = v.reshape(B, T, N_KV_HEADS, D).transpose(0, 2, 1, 3)
    scale = D ** -0.5

    def block(qblk):
        s = jnp.einsum('bkgqd,bkjd->bkgqj', qblk, kt).astype(jnp.float32) * scale
        pr = jax.nn.softmax(s, axis=-1).astype(vt.dtype)
        return jnp.einsum('bkgqj,bkjd->bkgqd', pr, vt)

    o = lax.map(block, qb)
    return o.transpose(1, 0, 4, 2, 3, 5).reshape(B, T, GROUP_W)


def natten_mixer(p, rpb):
    B, T, _ = p.shape
    H, D = N_NAT_HEADS, HEAD_DIM
    rows = T // GRID_W
    wh = min(NAT_WIN_H, rows)
    q, k, v = jnp.split(p, [GROUP_W, 2 * GROUP_W], axis=-1)
    grid = lambda t: t.reshape(B, rows, GRID_W, H, D).transpose(0, 3, 1, 2, 4)
    qg, kg, vg = grid(q), grid(k), grid(v)
    cols = np.arange(GRID_W)
    col_start = np.clip(cols - NAT_WIN_W // 2, 0, GRID_W - NAT_WIN_W)
    col_idx = col_start[:, None] + np.arange(NAT_WIN_W)[None, :]
    dcol_idx = col_idx - cols[:, None] + (NAT_WIN_W - 1)
    rpb_cols = rpb[:, :, dcol_idx]
    scale = D ** -0.5

    def row_block(r):
        rs = jnp.clip(r - wh // 2, 0, rows - wh)
        k_rows = lax.dynamic_slice_in_dim(kg, rs, wh, axis=2)
        v_rows = lax.dynamic_slice_in_dim(vg, rs, wh, axis=2)
        k_win = k_rows[:, :, :, col_idx]
        v_win = v_rows[:, :, :, col_idx]
        q_row = lax.dynamic_index_in_dim(qg, r, axis=2, keepdims=False)
        drow_idx = rs + jnp.arange(wh) - r + (NAT_WIN_H - 1)
        bias = jnp.take(rpb_cols, drow_idx, axis=1).transpose(0, 2, 1, 3)
        s = jnp.einsum('bhqd,bhwqjd->bhqwj', q_row, k_win).astype(jnp.float32) * scale
        s = s + bias[None].astype(jnp.float32)
        pr = jax.nn.softmax(s.reshape(B, H, GRID_W, wh * NAT_WIN_W), axis=-1)
        pr = pr.reshape(B, H, GRID_W, wh, NAT_WIN_W).astype(v_win.dtype)
        return jnp.einsum('bhqwj,bhwqjd->bhqd', pr, v_win)

    o = lax.map(row_block, jnp.arange(rows))
    return o.transpose(1, 0, 3, 2, 4).reshape(B, T, GROUP_W)


def pool_mixer(p, pool_w, pool_scale):
    B, T, _ = p.shape
    xg = p.astype(jnp.float32).reshape(B, T, N_POOL_GROUPS, POOL_GROUP_DIM)
    cs = jnp.concatenate([jnp.zeros_like(xg[:, :1]), jnp.cumsum(xg, axis=1)], axis=1)
    t = np.arange(T)
    outs = []
    for gi, win in enumerate(POOL_WINDOWS):
        lo = np.clip(t - win // 2, 0, T)
        hi = np.clip(t + win - win // 2, 0, T)
        cnt = (hi - lo).astype(np.float32)[None, :, None]
        mean = (cs[:, hi, gi] - cs[:, lo, gi]) / cnt
        outs.append(mean - xg[:, :, gi])
    d = jnp.stack(outs, axis=2)
    y = jnp.einsum('btgc,gce->btge', d, pool_w).reshape(B, T, GROUP_W)
    return y * pool_scale


def mixing_sublayer(h, w_in, w_out, rwkv_mu, rwkv_w0, rwkv_w2, rwkv_a0, rwkv_a2, rwkv_g2,
                    rwkv_k_k, rwkv_k_a, rwkv_r_k, rwkv_lnx_w, rwkv_lnx_b,
                    gqa_q_norm, gqa_k_norm, nat_rpb, pool_w, pool_scale):
    p = h @ w_in
    p_a, p_b, p_c, p_d = jnp.split(p, [A_COLS, A_COLS + B_COLS, A_COLS + B_COLS + C_COLS], axis=-1)
    y_a = rwkv7_mixer(p_a, rwkv_mu, rwkv_w0, rwkv_w2, rwkv_a0, rwkv_a2, rwkv_g2,
                      rwkv_k_k, rwkv_k_a, rwkv_r_k, rwkv_lnx_w, rwkv_lnx_b)
    y_b = gqa_mixer(p_b, gqa_q_norm, gqa_k_norm)
    y_c = natten_mixer(p_c, nat_rpb)
    y_d = pool_mixer(p_d, pool_w, pool_scale)
    y = jnp.concatenate([y_a.astype(h.dtype), y_b.astype(h.dtype), y_c.astype(h.dtype),
                         y_d.astype(h.dtype)], axis=-1)
    return y @ w_out


def swiglu(h, w_gate, w_up, w_down):
    return (jax.nn.silu(h @ w_gate) * (h @ w_up)) @ w_down


def moe_ffn(h, router, w_gate, w_up, w_down):
    logits = (h @ router).astype(jnp.float32)
    top_v, top_i = lax.top_k(logits, TOP_K)
    gates = jax.nn.softmax(top_v, axis=-1)
    combine = jnp.sum(jax.nn.one_hot(top_i, N_EXPERTS, dtype=jnp.float32) * gates[..., None], axis=-2)
    combine = combine.astype(h.dtype)
    y = jnp.zeros_like(h)
    for e in range(N_EXPERTS):
        y = y + combine[..., e:e + 1] * swiglu(h, w_gate[e], w_up[e], w_down[e])
    return y


def trunk(x, c, weights):
    (w_ada, b_ada, norm_mix_g, norm_ffn_g, w_in, w_out, rwkv_mu, rwkv_w0, rwkv_w2, rwkv_a0,
     rwkv_a2, rwkv_g2, rwkv_k_k, rwkv_k_a, rwkv_r_k, rwkv_lnx_w, rwkv_lnx_b, gqa_q_norm,
     gqa_k_norm, nat_rpb, pool_w, pool_scale, ffn_w_gate, ffn_w_up, ffn_w_down, moe_router,
     moe_w_gate, moe_w_up, moe_w_down, final_norm_g) = weights
    nb = c.shape[0]
    for l in range(DEPTH):
        mod = (jax.nn.silu(c) @ w_ada[l] + b_ada[l]).reshape(nb, N_MOD, 1, D_MODEL)
        shift1, scale1, gate1, shift2, scale2, gate2 = (mod[:, i] for i in range(N_MOD))
        h = rmsnorm(x, norm_mix_g[l]) * (1.0 + scale1) + shift1
        y = mixing_sublayer(h, w_in[l], w_out[l], rwkv_mu[l], rwkv_w0[l], rwkv_w2[l], rwkv_a0[l],
                            rwkv_a2[l], rwkv_g2[l], rwkv_k_k[l], rwkv_k_a[l], rwkv_r_k[l],
                            rwkv_lnx_w[l], rwkv_lnx_b[l], gqa_q_norm[l], gqa_k_norm[l],
                            nat_rpb[l], pool_w[l], pool_scale[l])
        x = x + gate1 * y
        h = rmsnorm(x, norm_ffn_g[l]) * (1.0 + scale2) + shift2
        i = l // 2
        if l % 2 == 0:
            f = swiglu(h, ffn_w_gate[i], ffn_w_up[i], ffn_w_down[i])
        else:
            f = moe_ffn(h, moe_router[i], moe_w_gate[i], moe_w_up[i], moe_w_down[i])
        x = x + gate2 * f
    return rmsnorm(x, final_norm_g)


def setup_inputs(seed: int = 0) -> dict:
    key = jax.random.key(seed)
    ks = iter(jax.random.split(key, 48))
    nrm = lambda shape, s: jax.random.normal(next(ks), shape, jnp.float32) * s
    L = DEPTH
    return {
        'x_prompt': nrm((BATCH, SEQ, D_MODEL), 1.0),
        'x_sample': nrm((DEC_BATCH, DEC_SEQ, D_MODEL), 1.0),
        'c_prompt': nrm((BATCH, D_MODEL), 1.0),
        'c_sample': nrm((DEC_BATCH, D_MODEL), 1.0),
        'w_ada': nrm((L, D_MODEL, N_MOD * D_MODEL), 0.5 * D_MODEL ** -0.5),
        'b_ada': nrm((L, N_MOD * D_MODEL), 0.02),
        'norm_mix_g': 1.0 + nrm((L, D_MODEL), 0.02),
        'norm_ffn_g': 1.0 + nrm((L, D_MODEL), 0.02),
        'w_in': nrm((L, D_MODEL, IN_COLS), D_MODEL ** -0.5),
        'w_out': nrm((L, D_MIX, D_MODEL), D_MIX ** -0.5),
        'rwkv_mu': jax.random.uniform(next(ks), (L, A_COLS), jnp.float32),
        'rwkv_w0': -1.5 + nrm((L, 2, GROUP_W), 0.5),
        'rwkv_w2': nrm((L, 2, DECAY_LORA, GROUP_W), 0.5 * DECAY_LORA ** -0.5),
        'rwkv_a0': nrm((L, 2, GROUP_W), 0.3),
        'rwkv_a2': nrm((L, 2, AAA_LORA, GROUP_W), AAA_LORA ** -0.5),
        'rwkv_g2': nrm((L, GATE_LORA, GROUP_W), GATE_LORA ** -0.5),
        'rwkv_k_k': 0.85 + nrm((L, GROUP_W), 0.05),
        'rwkv_k_a': 1.0 + nrm((L, GROUP_W), 0.05),
        'rwkv_r_k': nrm((L, N_RWKV_HEADS, HEAD_DIM), 0.1),
        'rwkv_lnx_w': 1.0 + nrm((L, GROUP_W), 0.02),
        'rwkv_lnx_b': nrm((L, GROUP_W), 0.02),
        'gqa_q_norm': 1.0 + nrm((L, HEAD_DIM), 0.02),
        'gqa_k_norm': 1.0 + nrm((L, HEAD_DIM), 0.02),
        'nat_rpb': nrm((L, N_NAT_HEADS, 2 * NAT_WIN_H - 1, 2 * NAT_WIN_W - 1), 0.1),
        'pool_w': nrm((L, N_POOL_GROUPS, POOL_GROUP_DIM, POOL_GROUP_DIM), POOL_GROUP_DIM ** -0.5),
        'pool_scale': 1.0 + nrm((L, GROUP_W), 0.1),
        'ffn_w_gate': nrm((N_DENSE, D_MODEL, D_FF_DENSE), D_MODEL ** -0.5),
        'ffn_w_up': nrm((N_DENSE, D_MODEL, D_FF_DENSE), D_MODEL ** -0.5),
        'ffn_w_down': nrm((N_DENSE, D_FF_DENSE, D_MODEL), D_FF_DENSE ** -0.5),
        'moe_router': nrm((N_MOE, D_MODEL, N_EXPERTS), D_MODEL ** -0.5),
        'moe_w_gate': nrm((N_MOE, N_EXPERTS, D_MODEL, D_FF_EXPERT), D_MODEL ** -0.5),
        'moe_w_up': nrm((N_MOE, N_EXPERTS, D_MODEL, D_FF_EXPERT), D_MODEL ** -0.5),
        'moe_w_down': nrm((N_MOE, N_EXPERTS, D_FF_EXPERT, D_MODEL), D_FF_EXPERT ** -0.5),
        'final_norm_g': 1.0 + nrm((D_MODEL,), 0.02),
    }


def reference(x_prompt, x_sample, c_prompt, c_sample, w_ada, b_ada, norm_mix_g, norm_ffn_g,
              w_in, w_out, rwkv_mu, rwkv_w0, rwkv_w2, rwkv_a0, rwkv_a2, rwkv_g2, rwkv_k_k,
              rwkv_k_a, rwkv_r_k, rwkv_lnx_w, rwkv_lnx_b, gqa_q_norm, gqa_k_norm, nat_rpb,
              pool_w, pool_scale, ffn_w_gate, ffn_w_up, ffn_w_down, moe_router, moe_w_gate,
              moe_w_up, moe_w_down, final_norm_g):
    weights = (w_ada, b_ada, norm_mix_g, norm_ffn_g, w_in, w_out, rwkv_mu, rwkv_w0, rwkv_w2,
               rwkv_a0, rwkv_a2, rwkv_g2, rwkv_k_k, rwkv_k_a, rwkv_r_k, rwkv_lnx_w, rwkv_lnx_b,
               gqa_q_norm, gqa_k_norm, nat_rpb, pool_w, pool_scale, ffn_w_gate, ffn_w_up,
               ffn_w_down, moe_router, moe_w_gate, moe_w_up, moe_w_down, final_norm_g)
    y_prompt = trunk(x_prompt, c_prompt, weights)
    y_sample = trunk(x_sample, c_sample, weights)
    return (y_prompt, y_sample)
```

```python
import functools

import jax
import jax.numpy as jnp
import numpy as np
from jax import lax
from jax.experimental import pallas as pl
from jax.experimental.pallas import tpu as pltpu

F32 = jnp.float32
BF16 = jnp.bfloat16

D_MODEL = 1024
DEPTH = 4
GRID_W = 64
HEAD_DIM = 64
GROUP_W = 256
N_HEADS = 4
N_KV_HEADS = 2
DECAY_LORA = 64
AAA_LORA = 64
GATE_LORA = 128
A_COLS = 3 * GROUP_W + 2 * DECAY_LORA + 2 * AAA_LORA + GATE_LORA
B_COLS = GROUP_W + 2 * N_KV_HEADS * HEAD_DIM
C_COLS = 3 * GROUP_W
D_COLS = GROUP_W
IN_COLS = A_COLS + B_COLS + C_COLS + D_COLS
ROPE_THETA = 10000.0
NAT_WIN_H = 8
NAT_WIN_W = 16
POOL_WINDOWS = (2, 4, 8, 16)
N_EXPERTS = 8
N_MOD = 6
NORM_EPS = 1e-6
GN_EPS = 64e-5
NEG_BIG = -1e30

LANES = 128
SUBLANES = 8
VMEM_LIMIT_BYTES = 56 * 1024 * 1024

RWKV_CHUNK = 64
RWKV_SUPER = 256
TOKEN_TILE = 512
FFN_TOKEN_TILE = 1024
FFN_DENSE_TILE = 256
FFN_EXPERT_TILE = 512
GQA_Q_TILE = 256
NAT_ROW_BLOCK = 8
POOL_TILE = 1024
POOL_HALO = 8


def _cparams(semantics):
    return pltpu.CompilerParams(dimension_semantics=semantics,
                                vmem_limit_bytes=VMEM_LIMIT_BYTES)


def _sigmoid(x):
    return 1.0 / (1.0 + jnp.exp(-x))


def _dot(a, b):
    return jnp.dot(a, b, preferred_element_type=F32)


def _dot_nt(a, b):
    return lax.dot_general(a, b, (((1,), (1,)), ((), ())), preferred_element_type=F32)


def _dot_tn(a, b):
    return lax.dot_general(a, b, (((0,), (0,)), ((), ())), preferred_element_type=F32)


def _rms(x):
    return x * lax.rsqrt(jnp.mean(x * x, axis=-1, keepdims=True) + NORM_EPS)


def _norm_mod(x, g, shift, scale):
    return (_rms(x) * g) * (1.0 + scale) + shift


def _mod_kernel(c_ref, w_ref, b_ref, o_ref):
    c = c_ref[...]
    s = c * _sigmoid(c)
    o_ref[...] = jnp.dot(s, w_ref[...], precision=lax.Precision.HIGHEST,
                         preferred_element_type=F32) + b_ref[...]


def _modulation(c_all, w_ada, b_ada):
    nb = c_all.shape[0]
    n_out = N_MOD * D_MODEL
    tn = 1536
    out = pl.pallas_call(
        _mod_kernel,
        out_shape=jax.ShapeDtypeStruct((DEPTH, nb, n_out), F32),
        grid=(DEPTH, n_out // tn),
        in_specs=[
            pl.BlockSpec((nb, D_MODEL), lambda l, j: (0, 0)),
            pl.BlockSpec((None, D_MODEL, tn), lambda l, j: (l, 0, j)),
            pl.BlockSpec((None, 1, tn), lambda l, j: (l, 0, j)),
        ],
        out_specs=pl.BlockSpec((None, nb, tn), lambda l, j: (l, 0, j)),
        compiler_params=_cparams(("arbitrary", "arbitrary")),
        name="modulation",
    )(c_all, w_ada, b_ada.reshape(DEPTH, 1, n_out))
    return out.reshape(DEPTH, nb, N_MOD, D_MODEL)


def _inproj_kernel(x_ref, mod_ref, g_ref, w_ref, pa_ref, pb_ref, pc_ref, pd_ref):
    h = _norm_mod(x_ref[...], g_ref[...], mod_ref[0:1, :], mod_ref[1:2, :])
    hb = h.astype(BF16)
    o0, o1, o2 = A_COLS, A_COLS + B_COLS, A_COLS + B_COLS + C_COLS
    pa_ref[...] = _dot(hb, w_ref[:, 0:o0])
    pb_ref[...] = _dot(hb, w_ref[:, o0:o1])
    pc_ref[...] = _dot(hb, w_ref[:, o1:o2]).astype(BF16)
    pd_ref[...] = _dot(hb, w_ref[:, o2:IN_COLS])


def _inproj(x, mod, g, w_in_bf, l):
    B, T, _ = x.shape
    tm = TOKEN_TILE
    tok = lambda c: pl.BlockSpec((None, tm, c), lambda b, i: (b, i, 0))
    return pl.pallas_call(
        _inproj_kernel,
        out_shape=(jax.ShapeDtypeStruct((B, T, A_COLS), F32),
                   jax.ShapeDtypeStruct((B, T, B_COLS), F32),
                   jax.ShapeDtypeStruct((B, T, C_COLS), BF16),
                   jax.ShapeDtypeStruct((B, T, D_COLS), F32)),
        grid=(B, T // tm),
        in_specs=[
            tok(D_MODEL),
            pl.BlockSpec((None, None, N_MOD, D_MODEL), lambda b, i: (l, b, 0, 0)),
            pl.BlockSpec((None, 1, D_MODEL), lambda b, i: (l, 0, 0)),
            pl.BlockSpec((None, D_MODEL, IN_COLS), lambda b, i: (l, 0, 0)),
        ],
        out_specs=(tok(A_COLS), tok(B_COLS), tok(C_COLS), tok(D_COLS)),
        compiler_params=_cparams(("arbitrary", "arbitrary")),
        name="inproj",
    )(x, mod, g, w_in_bf)


def _rwkv_prep_kernel(p_ref, pprev_ref, pnext_ref, mu_ref, w0_ref, w2_ref, a0_ref, a2_ref,
                      g2_ref, kk_ref, ka_ref, rv_ref, kg_ref, lw_ref, kd_ref, aa_ref, *, nt):
    i = pl.program_id(1)
    p = p_ref[...]
    tm = p.shape[0]
    row = lax.broadcasted_iota(jnp.int32, p.shape, 0)
    prev_row = jnp.where(i > 0, pprev_ref[SUBLANES - 1:SUBLANES, :], 0.0)
    next_row = jnp.where(i < nt - 1, pnext_ref[0:1, :], 0.0)
    prev = jnp.where(row == 0, prev_row, pltpu.roll(p, 1, 0))
    nxt = jnp.where(row == tm - 1, next_row, pltpu.roll(p, tm - 1, 0))
    p = p + mu_ref[...] * (0.5 * (prev + nxt) - p)

    G = GROUP_W
    r, k, v = p[:, 0:G], p[:, G:2 * G], p[:, 2 * G:3 * G]
    o_w = 3 * G
    o_a = o_w + 2 * DECAY_LORA
    o_g = o_a + 2 * AAA_LORA
    g = _dot(_sigmoid(p[:, o_g:o_g + GATE_LORA]).astype(BF16), g2_ref[...])
    lws, kds, avs = [], [], []
    for d in range(2):
        wl = p[:, o_w + d * DECAY_LORA:o_w + (d + 1) * DECAY_LORA]
        al = p[:, o_a + d * AAA_LORA:o_a + (d + 1) * AAA_LORA]
        z = w0_ref[d:d + 1, :] + _dot(jnp.tanh(wl).astype(BF16), w2_ref[d])
        softplus_neg = jnp.maximum(-z, 0.0) + jnp.log(1.0 + jnp.exp(-jnp.abs(z)))
        lws.append(-jnp.exp(-softplus_neg - 0.5))
        a = _sigmoid(a0_ref[d:d + 1, :] + _dot(al.astype(BF16), a2_ref[d]))
        avs.append(a)
        kds.append(k * (1.0 + (a - 1.0) * ka_ref[...]))
    kk = k * kk_ref[...]
    for h in range(N_HEADS):
        sl = slice(h * HEAD_DIM, (h + 1) * HEAD_DIM)
        kkh = kk[:, sl]
        kkh = kkh * lax.rsqrt(jnp.maximum(jnp.sum(kkh * kkh, axis=-1, keepdims=True), 1e-24))
        rv_ref[h] = jnp.concatenate([r[:, sl], v[:, sl]], axis=1)
        kg_ref[h] = jnp.concatenate([kkh, g[:, sl]], axis=1)
        lw_ref[h] = jnp.concatenate([lws[0][:, sl], lws[1][:, sl]], axis=1)
        kd_ref[h] = jnp.concatenate([kds[0][:, sl], kds[1][:, sl]], axis=1)
        aa_ref[h] = jnp.concatenate([avs[0][:, sl], avs[1][:, sl]], axis=1)


def _rwkv_prep(p_a, W, l):
    B, T, _ = p_a.shape
    tm = TOKEN_TILE
    nt = T // tm
    hb = tm // SUBLANES
    n_halo = T // SUBLANES
    out_sds = jax.ShapeDtypeStruct((B, N_HEADS, T, LANES), F32)
    out_spec = pl.BlockSpec((None, N_HEADS, tm, LANES), lambda b, i: (b, 0, i, 0))
    lsel = lambda *shape: pl.BlockSpec((None,) + shape, lambda b, i: (l,) + (0,) * len(shape))
    return pl.pallas_call(
        functools.partial(_rwkv_prep_kernel, nt=nt),
        out_shape=(out_sds,) * 5,
        grid=(B, nt),
        in_specs=[
            pl.BlockSpec((None, tm, A_COLS), lambda b, i: (b, i, 0)),
            pl.BlockSpec((None, SUBLANES, A_COLS),
                         lambda b, i: (b, jnp.maximum(i * hb - 1, 0), 0)),
            pl.BlockSpec((None, SUBLANES, A_COLS),
                         lambda b, i: (b, jnp.minimum((i + 1) * hb, n_halo - 1), 0)),
            lsel(1, A_COLS), lsel(2, GROUP_W), lsel(2, DECAY_LORA, GROUP_W),
            lsel(2, GROUP_W), lsel(2, AAA_LORA, GROUP_W), lsel(GATE_LORA, GROUP_W),
            lsel(1, GROUP_W), lsel(1, GROUP_W),
        ],
        out_specs=(out_spec,) * 5,
        compiler_params=_cparams(("arbitrary", "arbitrary")),
        name="rwkv_prep",
    )(p_a, p_a, p_a, W["rwkv_mu"], W["rwkv_w0"], W["rwkv_w2_bf"], W["rwkv_a0"], W["rwkv_a2_bf"],
      W["rwkv_g2_bf"], W["rwkv_k_k"], W["rwkv_k_a"])


def _rwkv_consts():
    n = RWKV_SUPER
    idx = np.arange(n)
    same = (idx[:, None] // RWKV_CHUNK) == (idx[None, :] // RWKV_CHUNK)
    lower = same & (idx[None, :] <= idx[:, None])
    upper = same & (idx[None, :] >= idx[:, None])
    strict_l = same & (idx[None, :] < idx[:, None])
    strict_u = same & (idx[None, :] > idx[:, None])
    cums = np.stack([lower, upper, same]).astype(np.float32)
    masks = np.stack([strict_l, lower, strict_u, upper]).astype(np.float32)
    return jnp.asarray(cums, BF16), jnp.asarray(masks, F32)


def _rwkv_scan_kernel(rv_ref, kg_ref, lw_ref, kd_ref, aa_ref, cum_ref, mask_ref,
                      lnw_ref, lnb_ref, rk_ref, o_ref,
                      ry_ref, mn_ref, wb_ref, y_ref, *, T):
    n = RWKV_SUPER
    C = RWKV_CHUNK
    n_super = T // n
    n_chunk = T // C
    cps = n // C
    lane = lax.broadcasted_iota(jnp.int32, (n, LANES), 1)
    lo_half = lane < HEAD_DIM
    eye = (lax.broadcasted_iota(jnp.int32, (C, LANES), 0)
           == (lax.broadcasted_iota(jnp.int32, (C, LANES), 1) % HEAD_DIM))

    def phase1(j, carry):
        rows = pl.ds(pl.multiple_of(j * n, n), n)
        RV, KG, LW, KD, AA = (ref[rows, :] for ref in (rv_ref, kg_ref, lw_ref, kd_ref, aa_ref))
        lw_hi = LW.astype(BF16)
        lw_lo = (LW - lw_hi.astype(F32)).astype(BF16)
        csum = lambda m: _dot(cum_ref[m], lw_hi) + _dot(cum_ref[m], lw_lo)
        cum = jnp.where(lo_half, csum(0), csum(1))
        tot = csum(2)
        RVs = pltpu.roll(RV, HEAD_DIM, 1)
        kk2 = jnp.where(lo_half, KG, pltpu.roll(KG, HEAD_DIM, 1))
        r2 = jnp.where(lo_half, RV, RVs)
        at = -(kk2 * jnp.exp(cum - LW))
        rt = r2 * jnp.exp(cum)
        w_inv = jnp.exp(-cum)
        w_hat = jnp.exp(tot - cum)
        w_tot = jnp.exp(tot)
        ka = kk2 * AA
        bt = (ka * w_inv).astype(BF16)
        kt = (KD * w_inv).astype(BF16)
        bh = ka * w_hat
        kh = KD * w_hat
        for d in range(2):
            mine = lo_half if d == 0 else jnp.logical_not(lo_half)
            at_d = jnp.where(mine, at, 0.0)
            rt_d = jnp.where(mine, rt, 0.0)
            at_b = at_d.astype(BF16)
            rt_b = rt_d.astype(BF16)
            v_pad = (jnp.where(lo_half, 0.0, RV) if d == 0 else jnp.where(lo_half, RVs, 0.0)).astype(BF16)
            m_strict = mask_ref[2 * d]
            m_incl = mask_ref[2 * d + 1]
            a_ab = _dot_nt(at_b, bt) * m_strict
            a_ak = (_dot_nt(at_b, kt) * m_strict).astype(BF16)
            a_rb = (_dot_nt(rt_b, bt) * m_incl).astype(BF16)
            a_rk = (_dot_nt(rt_b, kt) * m_incl).astype(BF16)
            X = at_d + _dot(a_ak, v_pad)
            P = a_ab
            for it in range(6):
                Pb = P.astype(BF16)
                X = X + _dot(Pb, X.astype(BF16))
                if it < 5:
                    P = _dot(Pb, Pb)
            Xb = X.astype(BF16)
            ry_ref[d, rows, :] = _dot(a_rb, Xb) + rt_d + _dot(a_rk, v_pad)
            bh_d = jnp.where(mine, bh, 0.0).astype(BF16)
            kh_d = jnp.where(mine, kh, 0.0).astype(BF16)
            for c in range(cps):
                cs = slice(c * C, (c + 1) * C)
                crow = pl.ds(pl.multiple_of(j * n + c * C, C), C)
                mn_full = _dot_tn(bh_d[cs], Xb[cs]) + _dot_tn(kh_d[cs], v_pad[cs])
                mn_ref[d, crow, :] = mn_full[d * HEAD_DIM:(d + 1) * HEAD_DIM, :]
                wt_c = w_tot[c * C:c * C + 1, :]
                if d == 1:
                    wt_c = pltpu.roll(jnp.broadcast_to(wt_c, (SUBLANES, LANES)), HEAD_DIM, 1)[0:1, :]
                wcol = jnp.sum(jnp.where(eye & (lax.broadcasted_iota(jnp.int32, (C, LANES), 1) < HEAD_DIM),
                                         wt_c, 0.0), axis=1, keepdims=True)
                wb_ref[d, crow, :] = jnp.broadcast_to(wcol, (C, LANES))
        return carry

    lax.fori_loop(0, n_super, phase1, 0)

    lane_c = lax.broadcasted_iota(jnp.int32, (C, LANES), 1)
    lo_c = lane_c < HEAD_DIM
    zeros_half = jnp.zeros((C, LANES), F32)

    def step(d, chunk, S):
        crow = pl.ds(pl.multiple_of(chunk * C, C), C)
        RY = ry_ref[d, crow, :]
        MN = mn_ref[d, crow, :]
        WB = wb_ref[d, crow, :]
        k_half = lo_c if d == 0 else jnp.logical_not(lo_c)
        Sb = S.astype(BF16)
        y = _dot(jnp.where(k_half, RY, 0.0).astype(BF16), Sb) + jnp.where(k_half, 0.0, RY)
        y_ref[d, crow, :] = y
        S_valid = S[0:C] if d == 0 else S[C:2 * C]
        new = WB * S_valid + _dot(jnp.where(k_half, MN, 0.0).astype(BF16), Sb) + jnp.where(k_half, 0.0, MN)
        if d == 0:
            return jnp.concatenate([new, zeros_half], axis=0)
        return jnp.concatenate([zeros_half, new], axis=0)

    def phase2(j, carry):
        s_f, s_b = carry
        return step(0, j, s_f), step(1, n_chunk - 1 - j, s_b)

    s0 = jnp.zeros((2 * C, LANES), F32)
    lax.fori_loop(0, n_chunk, phase2, (s0, s0))

    h = pl.program_id(1)
    te = TOKEN_TILE

    def phase3(j, carry):
        rows = pl.ds(pl.multiple_of(j * te, te), te)
        ypk = y_ref[0, rows, :] + y_ref[1, rows, :]
        y = ypk + pltpu.roll(ypk, HEAD_DIM, 1)
        mean = jnp.mean(y, axis=-1, keepdims=True)
        yc = y - mean
        var = jnp.mean(yc * yc, axis=-1, keepdims=True)
        yn = yc * lax.rsqrt(var + GN_EPS) * lnw_ref[...] + lnb_ref[...]
        RV = rv_ref[rows, :]
        KD = kd_ref[rows, :]
        KG = kg_ref[rows, :]
        lane_e = lax.broadcasted_iota(jnp.int32, (te, LANES), 1)
        r2 = jnp.where(lane_e < HEAD_DIM, RV, pltpu.roll(RV, HEAD_DIM, 1))
        ksum = KD + pltpu.roll(KD, HEAD_DIM, 1)
        bonus = 0.5 * jnp.sum(r2 * ksum * rk_ref[...], axis=-1, keepdims=True)
        res = ((yn + bonus * RV) * KG)[:, HEAD_DIM:].astype(o_ref.dtype)
        for hh in range(N_HEADS):
            @pl.when(h == hh)
            def _():
                o_ref[rows, hh * HEAD_DIM:(hh + 1) * HEAD_DIM] = res
        return carry

    lax.fori_loop(0, T // te, phase3, 0)


def _rwkv_scan(prep, W, l):
    rv = prep[0]
    B, _, T, _ = rv.shape
    cums, masks = _rwkv_consts()
    seq = pl.BlockSpec((None, None, T, LANES), lambda b, h: (b, h, 0, 0))
    per_head = pl.BlockSpec((None, None, 1, LANES), lambda b, h: (l, h, 0, 0))
    n = RWKV_SUPER
    return pl.pallas_call(
        functools.partial(_rwkv_scan_kernel, T=T),
        out_shape=jax.ShapeDtypeStruct((B, T, GROUP_W), BF16),
        grid=(B, N_HEADS),
        in_specs=[seq] * 5 + [
            pl.BlockSpec((3, n, n), lambda b, h: (0, 0, 0)),
            pl.BlockSpec((4, n, n), lambda b, h: (0, 0, 0)),
            per_head, per_head, per_head,
        ],
        out_specs=pl.BlockSpec((None, T, GROUP_W), lambda b, h: (b, 0, 0)),
        scratch_shapes=[pltpu.VMEM((2, T, LANES), F32)] * 4,
        compiler_params=_cparams(("arbitrary", "arbitrary")),
        name="rwkv_scan",
    )(*prep, cums, masks, W["rwkv_lnx_w2"], W["rwkv_lnx_b2"], W["rwkv_r_k2"])


def _rope_tables(T):
    t = jnp.arange(T)
    row = (t // GRID_W).astype(F32)
    col = (t % GRID_W).astype(F32)
    n_freq = HEAD_DIM // 4
    inv = ROPE_THETA ** (-jnp.arange(n_freq, dtype=F32) / n_freq)
    ang = jnp.concatenate([row[:, None] * inv, col[:, None] * inv], axis=-1)
    cos = jnp.repeat(jnp.cos(ang), 2, axis=-1)
    sin = jnp.repeat(jnp.sin(ang), 2, axis=-1)
    sign = jnp.tile(jnp.asarray([-1.0, 1.0], F32), HEAD_DIM // 2)
    return jnp.tile(cos, (1, N_HEADS)), jnp.tile(sin * sign, (1, N_HEADS))


def _gqa_prep_kernel(p_ref, cos_ref, sin_ref, bd_ref, qg_ref, kg_ref, q_ref, k_ref, v_ref):
    p = p_ref[...]
    G = GROUP_W
    kvw = N_KV_HEADS * HEAD_DIM

    def norm_rope(x, gain, width, scale):
        sq = x * x
        hi = sq.astype(BF16)
        lo = (sq - hi.astype(F32)).astype(BF16)
        bd = bd_ref[0:width, 0:width]
        ms = (_dot(hi, bd) + _dot(lo, bd)) * (1.0 / HEAD_DIM)
        xn = x * lax.rsqrt(ms + NORM_EPS) * gain
        lane = lax.broadcasted_iota(jnp.int32, xn.shape, 1)
        partner = jnp.where(lane % 2 == 0, pltpu.roll(xn, width - 1, 1), pltpu.roll(xn, 1, 1))
        return (xn * cos_ref[:, 0:width] + partner * sin_ref[:, 0:width]) * scale

    q = norm_rope(p[:, 0:G], qg_ref[...], G, HEAD_DIM ** -0.5)
    k = norm_rope(p[:, G:G + kvw], kg_ref[...], kvw, 1.0)
    v = p[:, G + kvw:G + 2 * kvw]
    for h in range(N_HEADS):
        q_ref[h] = q[:, h * HEAD_DIM:(h + 1) * HEAD_DIM].astype(BF16)
    for h in range(N_KV_HEADS):
        k_ref[h] = k[:, h * HEAD_DIM:(h + 1) * HEAD_DIM].astype(BF16)
        v_ref[h] = v[:, h * HEAD_DIM:(h + 1) * HEAD_DIM].astype(BF16)


def _gqa_prep(p_b, W, l, tables):
    B, T, _ = p_b.shape
    tm = TOKEN_TILE
    cos, sin = tables
    kvw = N_KV_HEADS * HEAD_DIM
    hm = lambda nh: pl.BlockSpec((None, nh, tm, HEAD_DIM), lambda b, i: (b, 0, i, 0))
    return pl.pallas_call(
        _gqa_prep_kernel,
        out_shape=(jax.ShapeDtypeStruct((B, N_HEADS, T, HEAD_DIM), BF16),
                   jax.ShapeDtypeStruct((B, N_KV_HEADS, T, HEAD_DIM), BF16),
                   jax.ShapeDtypeStruct((B, N_KV_HEADS, T, HEAD_DIM), BF16)),
        grid=(B, T // tm),
        in_specs=[
            pl.BlockSpec((None, tm, B_COLS), lambda b, i: (b, i, 0)),
            pl.BlockSpec((tm, GROUP_W), lambda b, i: (i, 0)),
            pl.BlockSpec((tm, GROUP_W), lambda b, i: (i, 0)),
            pl.BlockSpec((GROUP_W, GROUP_W), lambda b, i: (0, 0)),
            pl.BlockSpec((None, 1, GROUP_W), lambda b, i: (l, 0, 0)),
            pl.BlockSpec((None, 1, kvw), lambda b, i: (l, 0, 0)),
        ],
        out_specs=(hm(N_HEADS), hm(N_KV_HEADS), hm(N_KV_HEADS)),
        compiler_params=_cparams(("arbitrary", "arbitrary")),
        name="gqa_prep",
    )(p_b, cos, sin, W["head_blockdiag"], W["gqa_q_norm4"], W["gqa_k_norm2"])


def _gqa_kernel(q_ref, k_ref, v_ref, o_ref):
    g, tq, _ = q_ref.shape
    q = q_ref[...].reshape(g * tq, HEAD_DIM)
    s = _dot_nt(q, k_ref[...])
    m = jnp.max(s, axis=-1, keepdims=True)
    p = jnp.exp(s - m)
    denom = jnp.sum(p, axis=-1, keepdims=True)
    o = _dot(p.astype(BF16), v_ref[...]) / denom
    o_ref[...] = jnp.concatenate([o[i * tq:(i + 1) * tq] for i in range(g)], axis=1).astype(o_ref.dtype)


def _gqa(q, k, v):
    B, _, T, _ = q.shape
    tq = GQA_Q_TILE
    g = N_HEADS // N_KV_HEADS
    kv_spec = pl.BlockSpec((None, None, T, HEAD_DIM), lambda b, kv, i: (b, kv, 0, 0))
    return pl.pallas_call(
        _gqa_kernel,
        out_shape=jax.ShapeDtypeStruct((B, T, GROUP_W), BF16),
        grid=(B, N_KV_HEADS, T // tq),
        in_specs=[pl.BlockSpec((None, g, tq, HEAD_DIM), lambda b, kv, i: (b, kv, i, 0)), kv_spec, kv_spec],
        out_specs=pl.BlockSpec((None, tq, g * HEAD_DIM), lambda b, kv, i: (b, i, kv)),
        compiler_params=_cparams(("arbitrary", "arbitrary", "arbitrary")),
        name="gqa_attention",
    )(q, k, v)


def _nat_bias_table(rpb):
    q = np.arange(GRID_W)
    col_start = np.clip(q - NAT_WIN_W // 2, 0, GRID_W - NAT_WIN_W)
    c = np.arange(GRID_W)
    valid = (c[None, :] >= col_start[:, None]) & (c[None, :] < col_start[:, None] + NAT_WIN_W)
    dcol = np.clip(c[None, :] - q[:, None] + (NAT_WIN_W - 1), 0, 2 * NAT_WIN_W - 2)
    d = np.arange(NAT_WIN_H)
    w = np.arange(NAT_WIN_H)
    drow = w[None, :] - d[:, None] + (NAT_WIN_H - 1)
    tab = rpb[:, drow[:, :, None, None], dcol[None, None, :, :]]
    tab = jnp.where(jnp.asarray(valid)[None, None, None], tab, NEG_BIG)
    tab = tab.transpose(0, 1, 3, 2, 4)
    return tab.reshape(rpb.shape[0], NAT_WIN_H, GRID_W, NAT_WIN_H * GRID_W).astype(F32)


def _nat_kernel(q_ref, k_ref, v_ref, bias_ref, o_ref, *, rows):
    i = pl.program_id(1)
    band = NAT_WIN_H * GRID_W

    def body(jj, carry):
        r = i * NAT_ROW_BLOCK + jj
        rs = jnp.clip(r - NAT_WIN_H // 2, 0, rows - NAT_WIN_H)
        d = r - rs
        koff = pl.multiple_of(rs * GRID_W, GRID_W)
        qoff = pl.multiple_of(jj * GRID_W, GRID_W)
        kw = k_ref[pl.ds(koff, band), :]
        vw = v_ref[pl.ds(koff, band), :]
        q = q_ref[pl.ds(qoff, GRID_W), :]
        outs = []
        for h in range(N_HEADS):
            sl = slice(h * HEAD_DIM, (h + 1) * HEAD_DIM)
            s = _dot_nt(q[:, sl], kw[:, sl]) * (HEAD_DIM ** -0.5) + bias_ref[h, d]
            m = jnp.max(s, axis=-1, keepdims=True)
            p = jnp.exp(s - m)
            denom = jnp.sum(p, axis=-1, keepdims=True)
            outs.append(_dot(p.astype(BF16), vw[:, sl]) / denom)
        o_ref[pl.ds(qoff, GRID_W), :] = jnp.concatenate(outs, axis=1).astype(o_ref.dtype)
        return carry

    lax.fori_loop(0, NAT_ROW_BLOCK, body, 0)


def _natten(p_c, bias_tab):
    B, T, _ = p_c.shape
    rows = T // GRID_W
    tq = NAT_ROW_BLOCK * GRID_W
    band = NAT_WIN_H * GRID_W
    return pl.pallas_call(
        functools.partial(_nat_kernel, rows=rows),
        out_shape=jax.ShapeDtypeStruct((B, T, GROUP_W), BF16),
        grid=(B, T // tq),
        in_specs=[
            pl.BlockSpec((None, tq, GROUP_W), lambda b, i: (b, i, 0)),
            pl.BlockSpec((None, T, GROUP_W), lambda b, i: (b, 0, 1)),
            pl.BlockSpec((None, T, GROUP_W), lambda b, i: (b, 0, 2)),
            pl.BlockSpec((N_HEADS, NAT_WIN_H, GRID_W, band), lambda b, i: (0, 0, 0, 0)),
        ],
        out_specs=pl.BlockSpec((None, tq, GROUP_W), lambda b, i: (b, i, 0)),
        compiler_params=_cparams(("arbitrary", "arbitrary")),
        name="natten",
    )(p_c, p_c, p_c, bias_tab)


def _pool_kernel(x_ref, xp_ref, xn_ref, w_ref, sc_ref, o_ref, buf_ref, *, nt, T):
    i = pl.program_id(1)
    tm = x_ref.shape[0]
    H = POOL_HALO
    n = tm + 2 * H
    buf_ref[0:H, :] = jnp.where(i > 0, xp_ref[...], 0.0)
    buf_ref[H:H + tm, :] = x_ref[...]
    buf_ref[H + tm:n, :] = jnp.where(i < nt - 1, xn_ref[...], 0.0)
    xb = buf_ref[...]
    sh = lambda a, s: pltpu.roll(a, s % n, 0)
    s2 = sh(xb, 1) + xb
    s4 = sh(s2, 1) + sh(s2, -1)
    s8 = sh(s4, 2) + sh(s4, -2)
    s16 = sh(s8, 4) + sh(s8, -4)
    lane = lax.broadcasted_iota(jnp.int32, (tm, GROUP_W), 1)
    grp = lane // (GROUP_W // len(POOL_WINDOWS))
    body = lambda a: a[H:H + tm, :]
    win_sum = jnp.where(grp == 0, body(s2), jnp.where(grp == 1, body(s4),
                        jnp.where(grp == 2, body(s8), body(s16))))
    half = jnp.where(grp == 0, 1, jnp.where(grp == 1, 2, jnp.where(grp == 2, 4, 8)))
    t = i * tm + lax.broadcasted_iota(jnp.int32, (tm, GROUP_W), 0)
    cnt = (jnp.minimum(t + half, T) - jnp.maximum(t - half, 0)).astype(F32)
    dlt = win_sum / cnt - x_ref[...]
    o_ref[...] = (_dot(dlt.astype(BF16), w_ref[...]) * sc_ref[...]).astype(o_ref.dtype)


def _pool(p_d, W, l):
    B, T, _ = p_d.shape
    tm = POOL_TILE
    nt = T // tm
    hb = tm // POOL_HALO
    n_halo = T // POOL_HALO
    return pl.pallas_call(
        functools.partial(_pool_kernel, nt=nt, T=T),
        out_shape=jax.ShapeDtypeStruct((B, T, GROUP_W), BF16),
        grid=(B, nt),
        in_specs=[
            pl.BlockSpec((None, tm, GROUP_W), lambda b, i: (b, i, 0)),
            pl.BlockSpec((None, POOL_HALO, GROUP_W), lambda b, i: (b, jnp.maximum(i * hb - 1, 0), 0)),
            pl.BlockSpec((None, POOL_HALO, GROUP_W),
                         lambda b, i: (b, jnp.minimum((i + 1) * hb, n_halo - 1), 0)),
            pl.BlockSpec((None, GROUP_W, GROUP_W), lambda b, i: (l, 0, 0)),
            pl.BlockSpec((None, 1, GROUP_W), lambda b, i: (l, 0, 0)),
        ],
        out_specs=pl.BlockSpec((None, tm, GROUP_W), lambda b, i: (b, i, 0)),
        scratch_shapes=[pltpu.VMEM((tm + 2 * POOL_HALO, GROUP_W), F32)],
        compiler_params=_cparams(("arbitrary", "arbitrary")),
        name="pool",
    )(p_d, p_d, p_d, W["pool_w_bd"], W["pool_scale"])


def _outproj_kernel(x_ref, ya_ref, yb_ref, yc_ref, yd_ref, w_ref, mod_ref, o_ref):
    G = GROUP_W
    y = _dot(ya_ref[...], w_ref[0:G, :])
    y += _dot(yb_ref[...], w_ref[G:2 * G, :])
    y += _dot(yc_ref[...], w_ref[2 * G:3 * G, :])
    y += _dot(yd_ref[...], w_ref[3 * G:4 * G, :])
    o_ref[...] = x_ref[...] + mod_ref[2:3, :] * y


def _outproj(x, ys, w_out_bf, mod, l):
    B, T, _ = x.shape
    tm = TOKEN_TILE
    tok = lambda c: pl.BlockSpec((None, tm, c), lambda b, i: (b, i, 0))
    return pl.pallas_call(
        _outproj_kernel,
        out_shape=jax.ShapeDtypeStruct((B, T, D_MODEL), F32),
        grid=(B, T // tm),
        in_specs=[tok(D_MODEL)] + [tok(GROUP_W)] * 4 + [
            pl.BlockSpec((None, D_MODEL, D_MODEL), lambda b, i: (l, 0, 0)),
            pl.BlockSpec((None, None, N_MOD, D_MODEL), lambda b, i: (l, b, 0, 0)),
        ],
        out_specs=tok(D_MODEL),
        compiler_params=_cparams(("arbitrary", "arbitrary")),
        name="outproj",
    )(x, *ys, w_out_bf, mod)


def _ffn_finish(x, acc, gate, fg_ref, final):
    out = x + gate * acc
    if final:
        out = _rms(out) * fg_ref[...]
    return out


def _swiglu_partial(hb, wg_ref, wu_ref, wd_ref):
    g = _dot(hb, wg_ref[...])
    u = _dot(hb, wu_ref[...])
    a = (g * _sigmoid(g) * u).astype(BF16)
    return _dot(a, wd_ref[...])


def _ffn_kernel(x_ref, mod_ref, g_ref, wg_ref, wu_ref, wd_ref, fg_ref, o_ref, hb_ref, acc_ref,
                *, nf, final):
    j = pl.program_id(2)

    @pl.when(j == 0)
    def _():
        h = _norm_mod(x_ref[...], g_ref[...], mod_ref[3:4, :], mod_ref[4:5, :])
        hb_ref[...] = h.astype(BF16)
        acc_ref[...] = jnp.zeros_like(acc_ref)

    acc_ref[...] += _swiglu_partial(hb_ref[...], wg_ref, wu_ref, wd_ref)

    @pl.when(j == nf - 1)
    def _():
        o_ref[...] = _ffn_finish(x_ref[...], acc_ref[...], mod_ref[5:6, :], fg_ref, final)


def _ffn_dense(x, mod, g, W, l, final):
    B, T, _ = x.shape
    li = l // 2
    tm, tf = FFN_TOKEN_TILE, FFN_DENSE_TILE
    F = W["ffn_w_gate_bf"].shape[-1]
    nf = F // tf
    tok = pl.BlockSpec((None, tm, D_MODEL), lambda b, i, j: (b, i, 0))
    return pl.pallas_call(
        functools.partial(_ffn_kernel, nf=nf, final=final),
        out_shape=jax.ShapeDtypeStruct((B, T, D_MODEL), F32),
        grid=(B, T // tm, nf),
        in_specs=[
            tok,
            pl.BlockSpec((None, None, N_MOD, D_MODEL), lambda b, i, j: (l, b, 0, 0)),
            pl.BlockSpec((None, 1, D_MODEL), lambda b, i, j: (l, 0, 0)),
            pl.BlockSpec((None, D_MODEL, tf), lambda b, i, j: (li, 0, j)),
            pl.BlockSpec((None, D_MODEL, tf), lambda b, i, j: (li, 0, j)),
            pl.BlockSpec((None, tf, D_MODEL), lambda b, i, j: (li, j, 0)),
            pl.BlockSpec((1, D_MODEL), lambda b, i, j: (0, 0)),
        ],
        out_specs=tok,
        scratch_shapes=[pltpu.VMEM((tm, D_MODEL), BF16), pltpu.VMEM((tm, D_MODEL), F32)],
        compiler_params=_cparams(("arbitrary", "arbitrary", "arbitrary")),
        name="ffn_dense",
    )(x, mod, g, W["ffn_w_gate_bf"], W["ffn_w_up_bf"], W["ffn_w_down_bf"], W["final_norm_g"])


def _top2_combine(logits):
    lane = lax.broadcasted_iota(jnp.int32, logits.shape, 1)
    m1 = jnp.max(logits, axis=-1, keepdims=True)
    i1 = jnp.min(jnp.where(logits == m1, lane, LANES), axis=-1, keepdims=True)
    sel1 = lane == i1
    rest = jnp.where(sel1, NEG_BIG, logits)
    m2 = jnp.max(rest, axis=-1, keepdims=True)
    i2 = jnp.min(jnp.where(rest == m2, lane, LANES), axis=-1, keepdims=True)
    sel2 = lane == i2
    e2 = jnp.exp(m2 - m1)
    g1 = 1.0 / (1.0 + e2)
    g2 = e2 / (1.0 + e2)
    return jnp.where(sel1, g1, 0.0) + jnp.where(sel2, g2, 0.0)


def _moe_kernel(x_ref, mod_ref, g_ref, rt_ref, wg_ref, wu_ref, wd_ref, fg_ref, o_ref,
                hb_ref, comb_ref, acce_ref, acc_ref, *, nf, final):
    e = pl.program_id(2)
    j = pl.program_id(3)

    @pl.when((e == 0) & (j == 0))
    def _():
        h = _norm_mod(x_ref[...], g_ref[...], mod_ref[3:4, :], mod_ref[4:5, :])
        hb_ref[...] = h.astype(BF16)
        logits = jnp.dot(h, rt_ref[...], precision=lax.Precision.HIGHEST, preferred_element_type=F32)
        lane = lax.broadcasted_iota(jnp.int32, logits.shape, 1)
        comb_ref[...] = _top2_combine(jnp.where(lane < N_EXPERTS, logits, NEG_BIG))
        acc_ref[...] = jnp.zeros_like(acc_ref)

    @pl.when(j == 0)
    def _():
        acce_ref[...] = jnp.zeros_like(acce_ref)

    acce_ref[...] += _swiglu_partial(hb_ref[...], wg_ref, wu_ref, wd_ref)

    @pl.when(j == nf - 1)
    def _():
        lane = lax.broadcasted_iota(jnp.int32, comb_ref.shape, 1)
        ce = jnp.sum(jnp.where(lane == e, comb_ref[...], 0.0), axis=-1, keepdims=True)
        acc_ref[...] += ce * acce_ref[...]

    @pl.when((e == N_EXPERTS - 1) & (j == nf - 1))
    def _():
        o_ref[...] = _ffn_finish(x_ref[...], acc_ref[...], mod_ref[5:6, :], fg_ref, final)


def _ffn_moe(x, mod, g, W, l, final):
    B, T, _ = x.shape
    li = l // 2
    tm, tf = FFN_TOKEN_TILE, FFN_EXPERT_TILE
    F = W["moe_w_gate_bf"].shape[-1]
    nf = F // tf
    tok = pl.BlockSpec((None, tm, D_MODEL), lambda b, i, e, j: (b, i, 0))
    return pl.pallas_call(
        functools.partial(_moe_kernel, nf=nf, final=final),
        out_shape=jax.ShapeDtypeStruct((B, T, D_MODEL), F32),
        grid=(B, T // tm, N_EXPERTS, nf),
        in_specs=[
            tok,
            pl.BlockSpec((None, None, N_MOD, D_MODEL), lambda b, i, e, j: (l, b, 0, 0)),
            pl.BlockSpec((None, 1, D_MODEL), lambda b, i, e, j: (l, 0, 0)),
            pl.BlockSpec((None, D_MODEL, LANES), lambda b, i, e, j: (li, 0, 0)),
            pl.BlockSpec((None, None, D_MODEL, tf), lambda b, i, e, j: (li, e, 0, j)),
            pl.BlockSpec((None, None, D_MODEL, tf), lambda b, i, e, j: (li, e, 0, j)),
            pl.BlockSpec((None, None, tf, D_MODEL), lambda b, i, e, j: (li, e, j, 0)),
            pl.BlockSpec((1, D_MODEL), lambda b, i, e, j: (0, 0)),
        ],
        out_specs=tok,
        scratch_shapes=[pltpu.VMEM((tm, D_MODEL), BF16), pltpu.VMEM((tm, LANES), F32),
                        pltpu.VMEM((tm, D_MODEL), F32), pltpu.VMEM((tm, D_MODEL), F32)],
        compiler_params=_cparams(("arbitrary",) * 4),
        name="ffn_moe",
    )(x, mod, g, W["moe_router_pad"], W["moe_w_gate_bf"], W["moe_w_up_bf"], W["moe_w_down_bf"],
      W["final_norm_g"])


def _prepare_weights(w_in, w_out, rwkv_mu, rwkv_w0, rwkv_w2, rwkv_a0, rwkv_a2, rwkv_g2, rwkv_k_k,
                     rwkv_k_a, rwkv_r_k, rwkv_lnx_w, rwkv_lnx_b, gqa_q_norm, gqa_k_norm, nat_rpb,
                     pool_w, pool_scale, ffn_w_gate, ffn_w_up, ffn_w_down, moe_router, moe_w_gate,
                     moe_w_up, moe_w_down, final_norm_g, norm_mix_g, norm_ffn_g):
    L = DEPTH
    dup = lambda a: jnp.concatenate([a, a], axis=-1)
    per_head = lambda a: dup(a.reshape(L, N_HEADS, 1, HEAD_DIM))
    n_pool = len(POOL_WINDOWS)
    pool_bd = jnp.einsum('lgce,gh->lgche', pool_w, jnp.eye(n_pool, dtype=F32)).reshape(L, GROUP_W, GROUP_W)
    head_bd = np.kron(np.eye(N_HEADS, dtype=np.float32), np.ones((HEAD_DIM, HEAD_DIM), np.float32))
    return {
        "w_in_bf": w_in.astype(BF16),
        "w_out_bf": w_out.astype(BF16),
        "norm_mix_g": norm_mix_g.reshape(L, 1, D_MODEL),
        "norm_ffn_g": norm_ffn_g.reshape(L, 1, D_MODEL),
        "rwkv_mu": rwkv_mu.reshape(L, 1, A_COLS),
        "rwkv_w0": rwkv_w0,
        "rwkv_w2_bf": rwkv_w2.astype(BF16),
        "rwkv_a0": rwkv_a0,
        "rwkv_a2_bf": rwkv_a2.astype(BF16),
        "rwkv_g2_bf": rwkv_g2.astype(BF16),
        "rwkv_k_k": rwkv_k_k.reshape(L, 1, GROUP_W),
        "rwkv_k_a": rwkv_k_a.reshape(L, 1, GROUP_W),
        "rwkv_r_k2": per_head(rwkv_r_k),
        "rwkv_lnx_w2": per_head(rwkv_lnx_w),
        "rwkv_lnx_b2": per_head(rwkv_lnx_b),
        "gqa_q_norm4": jnp.tile(gqa_q_norm, (1, N_HEADS)).reshape(L, 1, GROUP_W),
        "gqa_k_norm2": jnp.tile(gqa_k_norm, (1, N_KV_HEADS)).reshape(L, 1, N_KV_HEADS * HEAD_DIM),
        "head_blockdiag": jnp.asarray(head_bd, BF16),
        "nat_bias": [_nat_bias_table(nat_rpb[l]) for l in range(L)],
        "pool_w_bd": pool_bd.astype(BF16),
        "pool_scale": pool_scale.reshape(L, 1, GROUP_W),
        "ffn_w_gate_bf": ffn_w_gate.astype(BF16),
        "ffn_w_up_bf": ffn_w_up.astype(BF16),
        "ffn_w_down_bf": ffn_w_down.astype(BF16),
        "moe_router_pad": jnp.pad(moe_router, ((0, 0), (0, 0), (0, LANES - N_EXPERTS))),
        "moe_w_gate_bf": moe_w_gate.astype(BF16),
        "moe_w_up_bf": moe_w_up.astype(BF16),
        "moe_w_down_bf": moe_w_down.astype(BF16),
        "final_norm_g": final_norm_g.reshape(1, D_MODEL),
    }


def _mixing(x, mod, W, l, tables):
    p_a, p_b, p_c, p_d = _inproj(x, mod, W["norm_mix_g"], W["w_in_bf"], l)
    y_a = _rwkv_scan(_rwkv_prep(p_a, W, l), W, l)
    y_b = _gqa(*_gqa_prep(p_b, W, l, tables))
    y_c = _natten(p_c, W["nat_bias"][l])
    y_d = _pool(p_d, W, l)
    return _outproj(x, (y_a, y_b, y_c, y_d), W["w_out_bf"], mod, l)


def _trunk(x, mod, W):
    tables = _rope_tables(x.shape[1])
    for l in range(DEPTH):
        x = _mixing(x, mod, W, l, tables)
        final = l == DEPTH - 1
        if l % 2 == 0:
            x = _ffn_dense(x, mod, W["norm_ffn_g"], W, l, final)
        else:
            x = _ffn_moe(x, mod, W["norm_ffn_g"], W, l, final)
    return x


def kernel(x_prompt, x_sample, c_prompt, c_sample, w_ada, b_ada, norm_mix_g, norm_ffn_g, w_in, w_out, rwkv_mu, rwkv_w0, rwkv_w2, rwkv_a0, rwkv_a2, rwkv_g2, rwkv_k_k, rwkv_k_a, rwkv_r_k, rwkv_lnx_w, rwkv_lnx_b, gqa_q_norm, gqa_k_norm, nat_rpb, pool_w, pool_scale, ffn_w_gate, ffn_w_up, ffn_w_down, moe_router, moe_w_gate, moe_w_up, moe_w_down, final_norm_g):
    W = _prepare_weights(w_in, w_out, rwkv_mu, rwkv_w0, rwkv_w2, rwkv_a0, rwkv_a2, rwkv_g2, rwkv_k_k,
                         rwkv_k_a, rwkv_r_k, rwkv_lnx_w, rwkv_lnx_b, gqa_q_norm, gqa_k_norm, nat_rpb,
                         pool_w, pool_scale, ffn_w_gate, ffn_w_up, ffn_w_down, moe_router, moe_w_gate,
                         moe_w_up, moe_w_down, final_norm_g, norm_mix_g, norm_ffn_g)
    nbp = c_prompt.shape[0]
    mod = _modulation(jnp.concatenate([c_prompt, c_sample], axis=0), w_ada, b_ada)
    y_prompt = _trunk(x_prompt, mod[:, :nbp], W)
    y_sample = _trunk(x_sample, mod[:, nbp:], W)
    return (y_prompt, y_sample)
```
